```python
import math
import jax, jax.numpy as jnp
from jax import lax
import numpy as np

D_MODEL = 1024
BATCH = 16
SEQ = 256
DEPTH = 4
DEC_BATCH = 2
DEC_SEQ = 2048
PAST_LEN = 256

GRID_W = 64
N_MIXERS = 3
N_A = (DEPTH + 2) // 3
N_B = (DEPTH + 1) // 3
N_C = DEPTH // 3
N_MOD = 9
D_FF = ((8 * D_MODEL // 3 + 127) // 128) * 128
EPS = 1e-6
H_A = D_MODEL // 128
DK_A = 128
DV_A = 128
INNER_A = H_A * DV_A
CONV_K = 5
CHUNK = 64
H_B = D_MODEL // 128
DH_B = 64
Q_BLOCK = 128
ROPE_BASE = 10000.0
POOL_WINDOWS = (2, 4, 8, 16)
POOL_GW = D_MODEL // 4

kernel_name = 'hybrid_diffusion_prefix_trunk_step'

F32 = jnp.float32


def _rms(x, g):
    xf = x.astype(F32)
    y = xf * lax.rsqrt(jnp.mean(xf * xf, axis=-1, keepdims=True) + EPS) * g.astype(F32)
    return y.astype(x.dtype)


def _l2(x):
    xf = x.astype(F32)
    return xf * lax.rsqrt(jnp.sum(xf * xf, axis=-1, keepdims=True) + EPS)


def _modulation(cond, w, b):
    m = jax.nn.silu(cond) @ w + b
    return m.reshape(cond.shape[0], 1, N_MOD, D_MODEL)


def _adanorm(x, g, shift, scale):
    return _rms(x, g) * (1.0 + scale) + shift


def _half_ffn(x, m, j0, g, wg, wu, wd):
    h = _adanorm(x, g, m[:, :, j0], m[:, :, j0 + 1])
    f = (jax.nn.silu(h @ wg) * (h @ wu)) @ wd
    return x + 0.5 * m[:, :, j0 + 2] * f


def _dwconv_centred(x, w):
    k, ch = w.shape
    return lax.conv_general_dilated(x, w[:, None, :].astype(x.dtype), window_strides=(1,),
                                    padding=[(k // 2, k // 2)],
                                    dimension_numbers=('NWC', 'WIO', 'NWC'),
                                    feature_group_count=ch)


def _chunk_gated_delta(q, k, v, g, beta, s0):
    b, l, h, dk = q.shape
    dv = v.shape[-1]
    n = l // CHUNK
    ch = lambda t: t.reshape(b, n, CHUNK, h, t.shape[-1]).transpose(1, 0, 3, 2, 4)
    cg = lambda t: t.reshape(b, n, CHUNK, h).transpose(1, 0, 3, 2)
    q, k, v = ch(q) * (dk ** -0.5), ch(k), ch(v)
    gcum = jnp.cumsum(cg(g), axis=-1)
    beta = cg(beta)
    idx = jnp.arange(CHUNK)
    incl = idx[:, None] >= idx[None, :]
    strict = idx[:, None] > idx[None, :]
    diff = gcum[..., :, None] - gcum[..., None, :]
    decay = jnp.where(incl, jnp.exp(jnp.where(incl, diff, 0.0)), 0.0)
    kb = k * beta[..., None]
    lmat = jnp.where(strict, jnp.einsum('nbhid,nbhjd->nbhij', kb, k) * decay, 0.0)
    amat = lmat + jnp.eye(CHUNK, dtype=F32)
    rhs = jnp.concatenate([v * beta[..., None], kb * jnp.exp(gcum)[..., None]], axis=-1)
    sol = lax.linalg.triangular_solve(amat, rhs, left_side=True, lower=True, unit_diagonal=True)
    u, w = sol[..., :dv], sol[..., dv:]
    attn = jnp.einsum('nbhid,nbhjd->nbhij', q, k) * decay
    qg = q * jnp.exp(gcum)[..., None]
    kd = k * jnp.exp(gcum[..., -1:] - gcum)[..., None]
    glast = jnp.exp(gcum[..., -1])

    def step(s, xs):
        u_i, w_i, attn_i, qg_i, kd_i, gl_i = xs
        v_new = u_i - jnp.einsum('bhcd,bhde->bhce', w_i, s)
        o = jnp.einsum('bhcd,bhde->bhce', qg_i, s) + jnp.einsum('bhij,bhje->bhie', attn_i, v_new)
        s = s * gl_i[..., None, None] + jnp.einsum('bhcd,bhce->bhde', kd_i, v_new)
        return s, o

    s_fin, o = lax.scan(step, s0, (u, w, attn, qg, kd, glast))
    o = o.transpose(1, 0, 3, 2, 4).reshape(b, l, h, dv)
    return o, s_fin


def _gated_deltanet(h, w_in, conv_w, a_log, dt_bias, norm_g, w_out, s0_f, s0_b):
    b, l, _ = h.shape
    proj = h @ w_in
    qkv = jax.nn.silu(_dwconv_centred(proj[..., :3 * INNER_A], conv_w))
    z = proj[..., 3 * INNER_A:4 * INNER_A].reshape(b, l, H_A, DV_A)
    ab = proj[..., 4 * INNER_A:].astype(F32).reshape(b, l, 2, 2, H_A)
    q = _l2(qkv[..., :INNER_A].reshape(b, l, H_A, DK_A))
    k = _l2(qkv[..., INNER_A:2 * INNER_A].reshape(b, l, H_A, DK_A))
    v = qkv[..., 2 * INNER_A:].astype(F32).reshape(b, l, H_A, DV_A)
    g = -jnp.exp(a_log.astype(F32)) * jax.nn.softplus(ab[:, :, :, 0] + dt_bias.astype(F32))
    beta = jax.nn.sigmoid(ab[:, :, :, 1])
    o_f, s_f = _chunk_gated_delta(q, k, v, g[:, :, 0], beta[:, :, 0], s0_f.astype(F32))
    rev = lambda t: jnp.flip(t, axis=1)
    o_b, s_b = _chunk_gated_delta(rev(q), rev(k), rev(v), rev(g[:, :, 1]), rev(beta[:, :, 1]),
                                  s0_b.astype(F32))
    o = o_f + rev(o_b)
    o = _rms(o, norm_g) * jax.nn.silu(z.astype(F32))
    y = o.reshape(b, l, INNER_A).astype(h.dtype) @ w_out
    return y, s_f, s_b


def _rope_1d(x, pos):
    nf = x.shape[-1] // 2
    inv = ROPE_BASE ** (-jnp.arange(nf, dtype=F32) / nf)
    ang = pos.astype(F32)[:, None] * inv
    cos = jnp.cos(ang)[None, :, None, None, :]
    sin = jnp.sin(ang)[None, :, None, None, :]
    xf = x.astype(F32)
    x1, x2 = xf[..., :nf], xf[..., nf:]
    return jnp.concatenate([x1 * cos - x2 * sin, x2 * cos + x1 * sin], axis=-1).astype(x.dtype)


def _axial_rope(x, row, col):
    half = x.shape[-1] // 2
    return jnp.concatenate([_rope_1d(x[..., :half], row), _rope_1d(x[..., half:], col)], axis=-1)


def _diff_qkv(h, w_qkv):
    b, l, _ = h.shape
    qkv = (h @ w_qkv).reshape(b, l, 3, H_B, 2, DH_B)
    return qkv[:, :, 0], qkv[:, :, 1], qkv[:, :, 2].reshape(b, l, H_B, 2 * DH_B)


def _diff_lambda(lam_p, lam_init):
    lp = lam_p.astype(F32)
    return jnp.exp(jnp.sum(lp[0] * lp[1])) - jnp.exp(jnp.sum(lp[2] * lp[3])) + lam_init


def _diff_attend(q, k, v, lam, lam_init, norm_g, w_out):
    b, lq = q.shape[:2]
    nb = lq // Q_BLOCK
    qb = jnp.moveaxis(q.reshape(b, nb, Q_BLOCK, H_B, 2, DH_B), 1, 0)
    scale = DH_B ** -0.5

    def blk(qi):
        s = jnp.einsum('bqhcd,bkhcd->bhcqk', qi, k).astype(F32) * scale
        p = jax.nn.softmax(s, axis=-1)
        a = (p[:, :, 0] - lam * p[:, :, 1]).astype(v.dtype)
        return jnp.einsum('bhqk,bkhe->bqhe', a, v)

    o = lax.map(blk, qb)
    o = jnp.moveaxis(o, 0, 1).reshape(b, lq, H_B, 2 * DH_B)
    o = _rms(o, norm_g) * (1.0 - lam_init)
    return o.reshape(b, lq, H_B * 2 * DH_B) @ w_out


def _pool_mixer(h, w_pool, scale):
    b, l, d = h.shape
    hf = h.astype(F32)
    cs = jnp.concatenate([jnp.zeros((b, 1, d), F32), jnp.cumsum(hf, axis=1)], axis=1)
    t = jnp.arange(l)
    parts = []
    for gi, win in enumerate(POOL_WINDOWS):
        sl = slice(gi * POOL_GW, (gi + 1) * POOL_GW)
        lo = jnp.clip(t - win // 2, 0, l)
        hi = jnp.clip(t + win // 2, 0, l)
        cnt = (hi - lo).astype(F32)[None, :, None]
        csg = cs[:, :, sl]
        parts.append((csg[:, hi] - csg[:, lo]) / cnt - hf[:, :, sl])
    dlt = jnp.stack(parts, axis=2)
    y = jnp.einsum('blgc,gce->blge', dlt, w_pool.astype(F32)).reshape(b, l, d) * scale.astype(F32)
    return y.astype(h.dtype)


def setup_inputs(seed: int = 0) -> dict:
    key = jax.random.key(seed)
    ks = iter(jax.random.split(key, 40))
    D = D_MODEL

    def nrm(shape, s=1.0):
        return jax.random.normal(next(ks), shape, F32) * s

    inp = {}
    inp['x_prompt'] = nrm((BATCH, SEQ, D))
    inp['x_sample'] = nrm((DEC_BATCH, DEC_SEQ, D))
    inp['state_a'] = nrm((DEC_BATCH, N_A, 2, H_A, DK_A, DV_A), 0.1)
    inp['cache_k'] = nrm((DEC_BATCH, N_B, PAST_LEN, H_B, 2 * DH_B))
    inp['cache_v'] = nrm((DEC_BATCH, N_B, PAST_LEN, H_B, 2 * DH_B))
    inp['c'] = nrm((DEC_BATCH, D))
    inp['c_ctx'] = nrm((D,))
    inp['w_mod'] = nrm((DEPTH, D, N_MOD * D), 0.5 * D ** -0.5)
    inp['b_mod'] = nrm((DEPTH, N_MOD * D), 0.01)
    inp['norm_g'] = 1.0 + nrm((DEPTH, 3, D), 0.02)
    inp['ffn_wg'] = nrm((DEPTH, 2, D, D_FF), D ** -0.5)
    inp['ffn_wu'] = nrm((DEPTH, 2, D, D_FF), D ** -0.5)
    inp['ffn_wd'] = nrm((DEPTH, 2, D_FF, D), D_FF ** -0.5)
    inp['a_w_in'] = nrm((N_A, D, 4 * INNER_A + 4 * H_A), D ** -0.5)
    inp['a_conv'] = nrm((N_A, CONV_K, 3 * INNER_A), CONV_K ** -0.5)
    a_val = jax.random.uniform(next(ks), (N_A, 2, H_A), F32, minval=1.0, maxval=16.0)
    inp['a_A_log'] = jnp.log(a_val)
    dt = jnp.exp(jax.random.uniform(next(ks), (N_A, 2, H_A), F32,
                                    minval=math.log(1e-3), maxval=math.log(1e-1)))
    inp['a_dt_bias'] = dt + jnp.log(-jnp.expm1(-dt))
    inp['a_norm_g'] = 1.0 + nrm((N_A, DV_A), 0.02)
    inp['a_w_out'] = nrm((N_A, INNER_A, D), INNER_A ** -0.5)
    inp['b_w_qkv'] = nrm((N_B, D, 3 * H_B * 2 * DH_B), D ** -0.5)
    inp['b_lam'] = nrm((N_B, 4, DH_B), 0.1)
    inp['b_norm_g'] = 1.0 + nrm((N_B, 2 * DH_B), 0.02)
    inp['b_w_out'] = nrm((N_B, H_B * 2 * DH_B, D), (H_B * 2 * DH_B) ** -0.5)
    inp['c_w_pool'] = nrm((N_C, 4, POOL_GW, POOL_GW), POOL_GW ** -0.5)
    inp['c_scale'] = 1.0 + nrm((N_C, D), 0.02)
    inp['final_g'] = 1.0 + nrm((D,), 0.02)
    return inp


def reference(x_prompt, x_sample, state_a, cache_k, cache_v, c, c_ctx, w_mod, b_mod, norm_g,
              ffn_wg, ffn_wu, ffn_wd, a_w_in, a_conv, a_A_log, a_dt_bias, a_norm_g, a_w_out,
              b_w_qkv, b_lam, b_norm_g, b_w_out, c_w_pool, c_scale, final_g):
    bp, lp, _ = x_prompt.shape
    bs, ls, _ = x_sample.shape
    rows = ls // GRID_W
    row_pos = jnp.repeat(jnp.arange(rows), GRID_W)
    col_pos = jnp.tile(jnp.arange(GRID_W), rows)
    xp, xs = x_prompt, x_sample
    new_sa, new_k, new_v = [], [], []
    for i in range(DEPTH):
        kind, slot = i % N_MIXERS, i // N_MIXERS
        mp = _modulation(c_ctx[None, :], w_mod[i], b_mod[i])
        ms = _modulation(c, w_mod[i], b_mod[i])
        xp = _half_ffn(xp, mp, 0, norm_g[i, 0], ffn_wg[i, 0], ffn_wu[i, 0], ffn_wd[i, 0])
        xs = _half_ffn(xs, ms, 0, norm_g[i, 0], ffn_wg[i, 0], ffn_wu[i, 0], ffn_wd[i, 0])
        hp = _adanorm(xp, norm_g[i, 1], mp[:, :, 3], mp[:, :, 4])
        hs = _adanorm(xs, norm_g[i, 1], ms[:, :, 3], ms[:, :, 4])
        if kind == 0:
            wa = (a_w_in[slot], a_conv[slot], a_A_log[slot], a_dt_bias[slot], a_norm_g[slot], a_w_out[slot])
            z0 = jnp.zeros((bp, H_A, DK_A, DV_A), F32)
            yp, s_f, s_b = _gated_deltanet(hp, *wa, z0, z0)
            ys, _, _ = _gated_deltanet(hs, *wa, state_a[:, slot, 0], state_a[:, slot, 1])
            new_sa.append(jnp.stack([s_f, s_b], axis=1))
        elif kind == 1:
            lam_init = 0.8 - 0.6 * math.exp(-0.3 * i)
            lam = _diff_lambda(b_lam[slot], lam_init)
            qp, kp, vp = _diff_qkv(hp, b_w_qkv[slot])
            yp = _diff_attend(qp, kp, vp, lam, lam_init, b_norm_g[slot], b_w_out[slot])
            qs, ks_, vs = _diff_qkv(hs, b_w_qkv[slot])
            qs = _axial_rope(qs, row_pos, col_pos)
            ks_ = _axial_rope(ks_, row_pos, col_pos)
            ck = cache_k[:, slot].reshape(bs, cache_k.shape[2], H_B, 2, DH_B).astype(ks_.dtype)
            k_all = jnp.concatenate([ks_, ck], axis=1)
            v_all = jnp.concatenate([vs, cache_v[:, slot].astype(vs.dtype)], axis=1)
            ys = _diff_attend(qs, k_all, v_all, lam, lam_init, b_norm_g[slot], b_w_out[slot])
            new_k.append(kp.reshape(bp, lp, H_B, 2 * DH_B))
            new_v.append(vp)
        else:
            yp = _pool_mixer(hp, c_w_pool[slot], c_scale[slot])
            ys = _pool_mixer(hs, c_w_pool[slot], c_scale[slot])
        xp = xp + mp[:, :, 5] * yp
        xs = xs + ms[:, :, 5] * ys
        xp = _half_ffn(xp, mp, 6, norm_g[i, 2], ffn_wg[i, 1], ffn_wu[i, 1], ffn_wd[i, 1])
        xs = _half_ffn(xs, ms, 6, norm_g[i, 2], ffn_wg[i, 1], ffn_wu[i, 1], ffn_wd[i, 1])
    y_prompt = _rms(xp, final_g)
    y_sample = _rms(xs, final_g)
    state_a_new = jnp.stack(new_sa, axis=1)
    cache_k_new = jnp.stack(new_k, axis=1)
    cache_v_new = jnp.stack(new_v, axis=1)
    return (y_prompt, y_sample, state_a_new, cache_k_new, cache_v_new)
```

```python
import functools
import math

import jax
import jax.numpy as jnp
from jax import lax
from jax.experimental import pallas as pl
from jax.experimental.pallas import tpu as pltpu

F32 = jnp.float32
BF16 = jnp.bfloat16

D_MODEL = 1024
BATCH = 16
SEQ = 256
DEPTH = 4
DEC_BATCH = 2
DEC_SEQ = 2048
PAST_LEN = 256
GRID_W = 64
N_MOD = 9
D_FF = 2816
EPS = 1e-6
N_HEADS = 8
HEAD_DIM = 128
DH_B = 64
CONV_K = 5
ROPE_BASE = 10000.0
POOL_WINDOWS = (2, 4, 8, 16)
POOL_GW = 256

P_ROWS = BATCH * SEQ
S_ROWS = DEC_BATCH * DEC_SEQ
N_ROWS = P_ROWS + S_ROWS
N_COND = 8

CHUNK = 128
VMEM_LIMIT = 56 * 1024 * 1024

NT_DIMS = (((1,), (1,)), ((), ()))
TN_DIMS = (((0,), (0,)), ((), ()))


def _cparams(*sem):
    return pltpu.CompilerParams(dimension_semantics=sem, vmem_limit_bytes=VMEM_LIMIT)


def _cond_idx(row_tile, tm):
    return jnp.maximum(row_tile * tm // DEC_SEQ - P_ROWS // DEC_SEQ + 1, 0)


def _mm(a, b):
    return jnp.dot(a, b, preferred_element_type=F32)


def _split(a):
    hi = a.astype(BF16)
    lo = (a - hi.astype(F32)).astype(BF16)
    return hi, lo


def _mm3(a, b):
    ah, al = _split(a)
    bh, bl = _split(b)
    return _mm(ah, bh) + (_mm(al, bh) + _mm(ah, bl))


def _rms(x, g):
    return x * lax.rsqrt(jnp.mean(x * x, axis=-1, keepdims=True) + EPS) * g


def _adanorm(x, g, shift, scale):
    return _rms(x, g) * (1.0 + scale) + shift


def _silu(x):
    return x * jax.nn.sigmoid(x)


def _mod_kernel(c_ref, w_ref, b_ref, o_ref):
    s = _silu(c_ref[...]).astype(BF16)
    o_ref[...] = _mm(s, w_ref[...].astype(BF16)) + b_ref[...]


def _modulation(cond, w_mod, b_mod):
    tn = 1024
    n = N_MOD * D_MODEL
    return pl.pallas_call(
        _mod_kernel,
        grid=(DEPTH, n // tn),
        in_specs=[
            pl.BlockSpec((N_COND, D_MODEL), lambda l, j: (0, 0)),
            pl.BlockSpec((None, D_MODEL, tn), lambda l, j: (l, 0, j)),
            pl.BlockSpec((None, 1, tn), lambda l, j: (l, 0, j)),
        ],
        out_specs=pl.BlockSpec((None, N_COND, tn), lambda l, j: (l, 0, j)),
        out_shape=jax.ShapeDtypeStruct((DEPTH, N_COND, n), F32),
        compiler_params=_cparams("parallel", "parallel"),
        name="modulation",
    )(cond, w_mod, b_mod.reshape(DEPTH, 1, n))


FFN_TM = 1024
FFN_TF = 256


def _ffn_kernel(x_ref, m_ref, g_ref, wg_ref, wu_ref, wd_ref, fg_ref, o_ref, h_ref, acc_ref,
                *, j0, final):
    k = pl.program_id(1)

    @pl.when(k == 0)
    def _():
        h = _adanorm(x_ref[...], g_ref[...], m_ref[j0:j0 + 1, :], m_ref[j0 + 1:j0 + 2, :])
        h_ref[...] = h.astype(BF16)
        acc_ref[...] = jnp.zeros_like(acc_ref)

    h = h_ref[...]
    g = _mm(h, wg_ref[...].astype(BF16))
    u = _mm(h, wu_ref[...].astype(BF16))
    a = (_silu(g) * u).astype(BF16)
    acc_ref[...] += _mm(a, wd_ref[...].astype(BF16))

    @pl.when(k == pl.num_programs(1) - 1)
    def _():
        y = x_ref[...] + 0.5 * m_ref[j0 + 2:j0 + 3, :] * acc_ref[...]
        if final:
            y = _rms(y, fg_ref[...])
        o_ref[...] = y


def _half_ffn(x, mod, norm_g4, ffn_wg, ffn_wu, ffn_wd, final_g, layer, half, j0, final):
    tm, tf = FFN_TM, FFN_TF
    return pl.pallas_call(
        functools.partial(_ffn_kernel, j0=j0, final=final),
        grid=(N_ROWS // tm, D_FF // tf),
        in_specs=[
            pl.BlockSpec((tm, D_MODEL), lambda i, k: (i, 0)),
            pl.BlockSpec((None, None, N_MOD, D_MODEL), lambda i, k: (layer, _cond_idx(i, tm), 0, 0)),
            pl.BlockSpec((None, None, 1, D_MODEL), lambda i, k: (layer, 2 * half, 0, 0)),
            pl.BlockSpec((None, None, D_MODEL, tf), lambda i, k: (layer, half, 0, k)),
            pl.BlockSpec((None, None, D_MODEL, tf), lambda i, k: (layer, half, 0, k)),
            pl.BlockSpec((None, None, tf, D_MODEL), lambda i, k: (layer, half, k, 0)),
            pl.BlockSpec((1, D_MODEL), lambda i, k: (0, 0)),
        ],
        out_specs=pl.BlockSpec((tm, D_MODEL), lambda i, k: (i, 0)),
        out_shape=jax.ShapeDtypeStruct((N_ROWS, D_MODEL), F32),
        scratch_shapes=[pltpu.VMEM((tm, D_MODEL), BF16), pltpu.VMEM((tm, D_MODEL), F32)],
        compiler_params=_cparams("parallel", "arbitrary"),
        name=f"half_ffn_l{layer}_{half}",
    )(x, mod, norm_g4, ffn_wg, ffn_wu, ffn_wd, final_g)


PROJ_TM = 512


def _proj_kernel(x_ref, m_ref, g_ref, w_ref, *rest, rope_tiles):
    if rope_tiles:
        cos_ref, sa_ref, sb_ref, o_ref, h_ref = rest
    else:
        o_ref, h_ref = rest
    j = pl.program_id(1)

    @pl.when(j == 0)
    def _():
        h = _adanorm(x_ref[...], g_ref[...], m_ref[3:4, :], m_ref[4:5, :])
        h_ref[...] = h.astype(BF16)

    y = _mm(h_ref[...], w_ref[...].astype(BF16))
    if not rope_tiles:
        o_ref[...] = y
        return

    @pl.when(j >= rope_tiles)
    def _():
        o_ref[...] = y

    @pl.when(j < rope_tiles)
    def _():
        cos, sa, sb = cos_ref[...], sa_ref[...], sb_ref[...]
        for s in range(y.shape[1] // HEAD_DIM):
            ys = y[:, s * HEAD_DIM:(s + 1) * HEAD_DIM]
            o_ref[:, s * HEAD_DIM:(s + 1) * HEAD_DIM] = (
                ys * cos + pltpu.roll(ys, 16, axis=1) * sa + pltpu.roll(ys, HEAD_DIM - 16, axis=1) * sb)


def _norm_proj(x, mod, norm_g4, w, layer, row0, rows, tn, rope=None):
    tm = PROJ_TM
    t0 = row0 // tm
    n = w.shape[1]
    in_specs = [
        pl.BlockSpec((tm, D_MODEL), lambda i, j: (i + t0, 0)),
        pl.BlockSpec((None, None, N_MOD, D_MODEL), lambda i, j: (layer, _cond_idx(i + t0, tm), 0, 0)),
        pl.BlockSpec((None, None, 1, D_MODEL), lambda i, j: (layer, 1, 0, 0)),
        pl.BlockSpec((D_MODEL, tn), lambda i, j: (0, j)),
    ]
    args = [x, mod, norm_g4, w]
    rope_tiles = 0
    if rope is not None:
        rope_tiles = 2 * D_MODEL // tn
        per_seq = DEC_SEQ // tm
        for t in rope:
            in_specs.append(pl.BlockSpec((tm, HEAD_DIM), lambda i, j: (i % per_seq, 0)))
            args.append(t)
    return pl.pallas_call(
        functools.partial(_proj_kernel, rope_tiles=rope_tiles),
        grid=(rows // tm, n // tn),
        in_specs=in_specs,
        out_specs=pl.BlockSpec((tm, tn), lambda i, j: (i, j)),
        out_shape=jax.ShapeDtypeStruct((rows, n), F32),
        scratch_shapes=[pltpu.VMEM((tm, D_MODEL), BF16)],
        compiler_params=_cparams("parallel", "arbitrary"),
        name=f"norm_proj_l{layer}_r{row0}_n{n}",
    )(*args)


def _gate_proj_kernel(x_ref, m_ref, g_ref, wt_ref, o_ref):
    h = _adanorm(x_ref[...], g_ref[...], m_ref[3:4, :], m_ref[4:5, :]).astype(BF16)
    o_ref[...] = lax.dot_general(wt_ref[...].astype(BF16), h, NT_DIMS, preferred_element_type=F32)


def _gate_proj(x, mod, norm_g4, w_t, layer):
    tm = PROJ_TM
    nc = w_t.shape[0]
    return pl.pallas_call(
        _gate_proj_kernel,
        grid=(N_ROWS // tm,),
        in_specs=[
            pl.BlockSpec((tm, D_MODEL), lambda i: (i, 0)),
            pl.BlockSpec((None, None, N_MOD, D_MODEL), lambda i: (layer, _cond_idx(i, tm), 0, 0)),
            pl.BlockSpec((None, None, 1, D_MODEL), lambda i: (layer, 1, 0, 0)),
            pl.BlockSpec((nc, D_MODEL), lambda i: (0, 0)),
        ],
        out_specs=pl.BlockSpec((nc, tm), lambda i: (0, i)),
        out_shape=jax.ShapeDtypeStruct((nc, N_ROWS), F32),
        compiler_params=_cparams("parallel"),
        name=f"gate_proj_l{layer}",
    )(x, mod, norm_g4, w_t)


def _norm_kernel(x_ref, m_ref, g_ref, o_ref):
    o_ref[...] = _adanorm(x_ref[...], g_ref[...], m_ref[3:4, :], m_ref[4:5, :])


def _norm_only(x, mod, norm_g4, layer):
    tm = PROJ_TM
    return pl.pallas_call(
        _norm_kernel,
        grid=(N_ROWS // tm,),
        in_specs=[
            pl.BlockSpec((tm, D_MODEL), lambda i: (i, 0)),
            pl.BlockSpec((None, None, N_MOD, D_MODEL), lambda i: (layer, _cond_idx(i, tm), 0, 0)),
            pl.BlockSpec((None, None, 1, D_MODEL), lambda i: (layer, 1, 0, 0)),
        ],
        out_specs=pl.BlockSpec((tm, D_MODEL), lambda i: (i, 0)),
        out_shape=jax.ShapeDtypeStruct((N_ROWS, D_MODEL), F32),
        compiler_params=_cparams("parallel"),
        name=f"norm_l{layer}",
    )(x, mod, norm_g4)


def _out_proj_kernel(o_ref, w_ref, x_ref, m_ref, y_ref):
    y = _mm(o_ref[...].astype(BF16), w_ref[...].astype(BF16))
    y_ref[...] = x_ref[...] + m_ref[5:6, :] * y


def _out_proj(o, w, x, mod, layer):
    tm = PROJ_TM
    return pl.pallas_call(
        _out_proj_kernel,
        grid=(N_ROWS // tm,),
        in_specs=[
            pl.BlockSpec((tm, D_MODEL), lambda i: (i, 0)),
            pl.BlockSpec((D_MODEL, D_MODEL), lambda i: (0, 0)),
            pl.BlockSpec((tm, D_MODEL), lambda i: (i, 0)),
            pl.BlockSpec((None, None, N_MOD, D_MODEL), lambda i: (layer, _cond_idx(i, tm), 0, 0)),
        ],
        out_specs=pl.BlockSpec((tm, D_MODEL), lambda i: (i, 0)),
        out_shape=jax.ShapeDtypeStruct((N_ROWS, D_MODEL), F32),
        compiler_params=_cparams("parallel"),
        name=f"out_proj_l{layer}",
    )(o, w, x, mod)


def _tri_inverse(lm, ii, jj):
    x = ii ^ jj
    eye = (x == 0).astype(F32)
    t = eye - jnp.where(x == 1, lm, 0.0)
    for lg in range(1, int(math.log2(CHUNK))):
        b = jnp.where((x >> lg) == 1, lm, 0.0)
        t = t - _mm3(t, _mm3(b, t))
    return t


def _delta_kernel(q_ref, k_ref, v_ref, z_ref, cq_ref, ck_ref, cv_ref, gt_ref, par_ref, ng_ref, *rest,
                  seq_len, has_state):
    if has_state:
        s0_ref, o_ref, sout_ref = rest[:3]
        scr = rest[3:]
    else:
        o_ref, sout_ref = rest[:2]
        scr = rest[2:]
    pad_ref, qs, ks, vs, gs, us, ws, at, qg, kd, gl, of, ob = scr
    L = seq_len
    n = L // CHUNK
    C = CHUNK

    zeros8 = jnp.zeros((8, HEAD_DIM), F32)
    pad_ref[0:8, :] = zeros8
    pad_ref[8 + L:16 + L, :] = zeros8

    def conv_silu(raw_ref, w_ref):
        pad_ref[8:8 + L, :] = raw_ref[...]
        xp = pad_ref[...]
        w = w_ref[...]
        acc = xp * w[2:3, :]
        for d in (-2, -1, 1, 2):
            acc = acc + pltpu.roll(xp, (-d) % (L + 16), axis=0) * w[d + 2:d + 3, :]
        return _silu(acc[8:8 + L, :])

    def l2(x):
        return x * lax.rsqrt(jnp.sum(x * x, axis=-1, keepdims=True) + EPS)

    qs[...] = l2(conv_silu(q_ref, cq_ref)) * (HEAD_DIM ** -0.5)
    ks[...] = l2(conv_silu(k_ref, ck_ref))
    vs[...] = conv_silu(v_ref, cv_ref)

    raw = gt_ref[...]
    par = par_ref[...]
    xa = raw + par[:, 1:2]
    softplus = jnp.maximum(xa, 0.0) + jnp.log1p(jnp.exp(-jnp.abs(xa)))
    g_all = -jnp.exp(par[:, 0:1]) * softplus
    b_all = jax.nn.sigmoid(raw)
    rid = lax.broadcasted_iota(jnp.int32, raw.shape, 0)
    gs[0:4, :] = jnp.where((rid & 1) == 0, g_all, b_all)

    ii = lax.broadcasted_iota(jnp.int32, (C, C), 0)
    jj = lax.broadcasted_iota(jnp.int32, (C, C), 1)
    ones = jnp.ones((C, C), BF16)

    def intra(c, d):
        r0 = pl.multiple_of(c * C, C)
        rows = pl.ds(r0, C)
        incl = (ii >= jj) if d == 0 else (ii <= jj)
        strict = (ii > jj) if d == 0 else (ii < jj)
        cum = jnp.where((ii <= jj) if d == 0 else (ii >= jj), 1.0, 0.0).astype(BF16)
        g8 = jnp.broadcast_to(gs[2 * d:2 * d + 1, rows], (8, C))
        b_col = jnp.broadcast_to(gs[2 * d + 1:2 * d + 2, rows], (C, C)).T
        g_hi = g8.astype(BF16)
        r1 = g8 - g_hi.astype(F32)
        g_mid = r1.astype(BF16)
        g_lo = (r1 - g_mid.astype(F32)).astype(BF16)
        cum2 = jnp.concatenate([cum, ones], axis=1)
        res = _mm(g_hi, cum2) + (_mm(g_mid, cum2) + _mm(g_lo, cum2))
        a_t = jnp.broadcast_to(res[0:1, :C], (C, C))
        g_tot = jnp.broadcast_to(res[0:1, C:], (C, C))
        a = a_t.T
        dec = jnp.where(incl, jnp.exp(jnp.where(incl, a - a_t, 0.0)), 0.0)
        qc, kc, vc = qs[rows, :], ks[rows, :], vs[rows, :]
        kb = kc * b_col
        kcb = kc.astype(BF16)
        kk = lax.dot_general(kb.astype(BF16), kcb, NT_DIMS, preferred_element_type=F32)
        t = _tri_inverse(jnp.where(strict, kk * dec, 0.0), ii, jj)
        eg = jnp.exp(a)
        th, tl = _split(t)
        rhs = jnp.concatenate([vc * b_col, kb * eg], axis=1)
        rh, rl = _split(rhs)
        sol = _mm(th, rh) + (_mm(tl, rh) + _mm(th, rl))
        us[d, rows, :] = sol[:, :C]
        ws[d, rows, :] = sol[:, C:]
        qk = lax.dot_general(qc.astype(BF16), kcb, NT_DIMS, preferred_element_type=F32)
        at[d, rows, :] = qk * dec
        qg[d, rows, :] = qc * eg
        kd[d, rows, :] = kc * jnp.exp(g_tot - a)
        gl[d, rows, :] = jnp.exp(g_tot)

    def intra_body(c, carry):
        intra(c, 0)
        intra(c, 1)
        return carry

    lax.fori_loop(0, n, intra_body, 0)

    def step(c, d, s):
        rows = pl.ds(pl.multiple_of(c * C, C), C)
        sb = s.astype(BF16)
        v_new = us[d, rows, :] - _mm(ws[d, rows, :].astype(BF16), sb)
        vb = v_new.astype(BF16)
        o = _mm(qg[d, rows, :].astype(BF16), sb) + _mm(at[d, rows, :].astype(BF16), vb)
        s = s * gl[d, rows, :] + lax.dot_general(kd[d, rows, :].astype(BF16), vb, TN_DIMS,
                                                 preferred_element_type=F32)
        return o, s

    def rec_body(i, carry):
        s_f, s_b = carry
        o_f, s_f = step(i, 0, s_f)
        of[pl.ds(pl.multiple_of(i * C, C), C), :] = o_f
        cb = n - 1 - i
        o_b, s_b = step(cb, 1, s_b)
        ob[pl.ds(pl.multiple_of(cb * C, C), C), :] = o_b
        return s_f, s_b

    if has_state:
        init = (s0_ref[0], s0_ref[1])
    else:
        init = (jnp.zeros((C, C), F32), jnp.zeros((C, C), F32))
    s_f, s_b = lax.fori_loop(0, n, rec_body, init)
    sout_ref[0] = s_f
    sout_ref[1] = s_b
    o = _rms(of[...] + ob[...], ng_ref[...]) * _silu(z_ref[...])
    o_ref[...] = o


def _delta_core(proj, gates_t, conv_w, par, norm_g, s0, row0, n_seq, seq_len):
    L = seq_len
    b0 = row0 // L
    H = N_HEADS
    has_state = s0 is not None
    in_specs = [
        pl.BlockSpec((L, HEAD_DIM), lambda b, h: (b + b0, h)),
        pl.BlockSpec((L, HEAD_DIM), lambda b, h: (b + b0, H + h)),
        pl.BlockSpec((L, HEAD_DIM), lambda b, h: (b + b0, 2 * H + h)),
        pl.BlockSpec((L, HEAD_DIM), lambda b, h: (b + b0, 3 * H + h)),
        pl.BlockSpec((CONV_K, HEAD_DIM), lambda b, h: (0, h)),
        pl.BlockSpec((CONV_K, HEAD_DIM), lambda b, h: (0, H + h)),
        pl.BlockSpec((CONV_K, HEAD_DIM), lambda b, h: (0, 2 * H + h)),
        pl.BlockSpec((None, 4, L), lambda b, h: (h, 0, b + b0)),
        pl.BlockSpec((None, 4, 2), lambda b, h: (h, 0, 0)),
        pl.BlockSpec((1, HEAD_DIM), lambda b, h: (0, 0)),
    ]
    args = [proj, proj, proj, proj, conv_w, conv_w, conv_w, gates_t, par, norm_g]
    if has_state:
        in_specs.append(pl.BlockSpec((None, 2, None, HEAD_DIM, HEAD_DIM), lambda b, h: (b, 0, h, 0, 0)))
        args.append(s0)
    big = lambda: pltpu.VMEM((2, L, HEAD_DIM), F32)
    one = lambda: pltpu.VMEM((L, HEAD_DIM), F32)
    scratch = [pltpu.VMEM((L + 16, HEAD_DIM), F32), one(), one(), one(), pltpu.VMEM((8, L), F32),
               big(), big(), big(), big(), big(), big(), one(), one()]
    return pl.pallas_call(
        functools.partial(_delta_kernel, seq_len=L, has_state=has_state),
        grid=(n_seq, H),
        in_specs=in_specs,
        out_specs=[
            pl.BlockSpec((L, HEAD_DIM), lambda b, h: (b, h)),
            pl.BlockSpec((None, 2, None, HEAD_DIM, HEAD_DIM), lambda b, h: (b, 0, h, 0, 0)),
        ],
        out_shape=[
            jax.ShapeDtypeStruct((n_seq * L, D_MODEL), F32),
            jax.ShapeDtypeStruct((n_seq, 2, H, HEAD_DIM, HEAD_DIM), F32),
        ],
        scratch_shapes=scratch,
        compiler_params=_cparams("parallel", "parallel"),
        name=f"delta_core_L{L}",
    )(*args)


def _attn_kernel(lam_ref, q_ref, k_ref, v_ref, *rest, lam_init, has_cache):
    if has_cache:
        ck_ref, cv_ref, ng_ref, o_ref = rest
    else:
        ng_ref, o_ref = rest
    lp = lam_ref[...]
    lam = (jnp.exp(jnp.sum(lp[0:1] * lp[1:2], axis=1, keepdims=True))
           - jnp.exp(jnp.sum(lp[2:3] * lp[3:4], axis=1, keepdims=True)) + lam_init)
    scale = DH_B ** -0.5
    q = q_ref[...]
    lane = lax.broadcasted_iota(jnp.int32, q.shape, 1)
    qc = (jnp.where(lane < DH_B, q, 0.0).astype(BF16), jnp.where(lane >= DH_B, q, 0.0).astype(BF16))
    keys = [k_ref[...].astype(BF16)]
    vals = [v_ref[...].astype(BF16)]
    if has_cache:
        keys.append(ck_ref[...].astype(BF16))
        vals.append(cv_ref[...].astype(BF16))
    probs = []
    for c in range(2):
        s = [lax.dot_general(qc[c], kk, NT_DIMS, preferred_element_type=F32) * scale for kk in keys]
        m = s[0].max(axis=-1, keepdims=True)
        for sp in s[1:]:
            m = jnp.maximum(m, sp.max(axis=-1, keepdims=True))
        e = [jnp.exp(sp - m) for sp in s]
        den = e[0].sum(axis=-1, keepdims=True)
        for ep in e[1:]:
            den = den + ep.sum(axis=-1, keepdims=True)
        inv = 1.0 / den
        probs.append([ep * inv for ep in e])
    o = None
    for p1, p2, vv in zip(probs[0], probs[1], vals):
        part = _mm((p1 - lam * p2).astype(BF16), vv)
        o = part if o is None else o + part
    o_ref[...] = _rms(o, ng_ref[...]) * (1.0 - lam_init)


def _attn_core(qkv, lam_p, norm_g, lam_init, n_seq, seq_len, tq, cache_k=None, cache_v=None):
    L = seq_len
    H = N_HEADS
    nq = L // tq
    has_cache = cache_k is not None
    in_specs = [
        pl.BlockSpec((4, DH_B), lambda b, h, i: (0, 0)),
        pl.BlockSpec((tq, HEAD_DIM), lambda b, h, i: (b * nq + i, h)),
        pl.BlockSpec((L, HEAD_DIM), lambda b, h, i: (b, H + h)),
        pl.BlockSpec((L, HEAD_DIM), lambda b, h, i: (b, 2 * H + h)),
    ]
    args = [lam_p, qkv, qkv, qkv]
    if has_cache:
        in_specs += [pl.BlockSpec((None, PAST_LEN, HEAD_DIM), lambda b, h, i: (b, 0, h))] * 2
        args += [cache_k, cache_v]
    in_specs.append(pl.BlockSpec((1, HEAD_DIM), lambda b, h, i: (0, 0)))
    args.append(norm_g)
    return pl.pallas_call(
        functools.partial(_attn_kernel, lam_init=lam_init, has_cache=has_cache),
        grid=(n_seq, H, nq),
        in_specs=in_specs,
        out_specs=pl.BlockSpec((tq, HEAD_DIM), lambda b, h, i: (b * nq + i, h)),
        out_shape=jax.ShapeDtypeStruct((n_seq * L, D_MODEL), F32),
        compiler_params=_cparams("parallel", "parallel", "parallel"),
        name=f"attn_core_L{L}",
    )(*args)


def _pool_kernel(h_ref, x_ref, w_ref, sc_ref, m_ref, o_ref, pad_ref, *, seq_len):
    L = seq_len
    Lp = L + 16
    gi = pl.program_id(1)
    zeros8 = jnp.zeros((8, POOL_GW), F32)
    pad_ref[0:8, :] = zeros8
    pad_ref[8 + L:Lp, :] = zeros8
    h = h_ref[...]
    pad_ref[8:8 + L, :] = h
    t = lax.broadcasted_iota(jnp.int32, (L, 1), 0)

    for g, win in enumerate(POOL_WINDOWS):
        @pl.when(gi == g)
        def _(win=win):
            half = win // 2
            p = pad_ref[...]
            span = 1
            while span < win:
                p = p + pltpu.roll(p, span, axis=0)
                span *= 2
            if half > 1:
                p = pltpu.roll(p, Lp - (half - 1), axis=0)
            tot = p[8:8 + L, :]
            cnt = (jnp.minimum(t + half, L) - jnp.maximum(t - half, 0)).astype(F32)
            dlt = tot / cnt - h
            y = _mm(dlt.astype(BF16), w_ref[...].astype(BF16)) * sc_ref[...]
            o_ref[...] = x_ref[...] + m_ref[5:6, :] * y


def _pool_mix(hn, x, w_pool, c_scale, mod, layer, row0, n_seq, seq_len, x_prev=None):
    L = seq_len
    b0 = row0 // L
    n_groups = len(POOL_WINDOWS)
    in_specs = [
        pl.BlockSpec((L, POOL_GW), lambda b, g: (b + b0, g)),
        pl.BlockSpec((L, POOL_GW), lambda b, g: (b + b0, g)),
        pl.BlockSpec((None, POOL_GW, POOL_GW), lambda b, g: (g, 0, 0)),
        pl.BlockSpec((1, POOL_GW), lambda b, g: (0, g)),
        pl.BlockSpec((None, None, N_MOD, POOL_GW), lambda b, g: (layer, _cond_idx(b + b0, L), 0, g)),
    ]
    return pl.pallas_call(
        functools.partial(_pool_kernel, seq_len=L),
        grid=(n_seq, n_groups),
        in_specs=in_specs,
        out_specs=pl.BlockSpec((L, POOL_GW), lambda b, g: (b, g)),
        out_shape=jax.ShapeDtypeStruct((n_seq * L, D_MODEL), F32),
        scratch_shapes=[pltpu.VMEM((L + 16, POOL_GW), F32)],
        compiler_params=_cparams("parallel", "parallel"),
        name=f"pool_mix_L{L}",
    )(hn, x, w_pool, c_scale, mod)


def _rope_tables():
    t = jnp.arange(DEC_SEQ)
    pos = jnp.stack([t // GRID_W, t % GRID_W], axis=1).astype(F32)
    lane = jnp.arange(HEAD_DIM)
    m = lane % DH_B
    part = m // 32
    r = m % 32
    nf = 16
    inv = ROPE_BASE ** (-(r % nf).astype(F32) / nf)
    ang = jnp.take(pos, part, axis=1) * inv[None, :]
    first = (r < nf)[None, :]
    sin = jnp.sin(ang)
    return jnp.cos(ang), jnp.where(first, 0.0, sin), jnp.where(first, -sin, 0.0)


def kernel(x_prompt, x_sample, state_a, cache_k, cache_v, c, c_ctx, w_mod, b_mod, norm_g, ffn_wg, ffn_wu,
           ffn_wd, a_w_in, a_conv, a_A_log, a_dt_bias, a_norm_g, a_w_out, b_w_qkv, b_lam, b_norm_g,
           b_w_out, c_w_pool, c_scale, final_g):
    x = jnp.concatenate([x_prompt.reshape(P_ROWS, D_MODEL), x_sample.reshape(S_ROWS, D_MODEL)], axis=0)
    cond = jnp.concatenate([c_ctx[None, :], c, jnp.zeros((N_COND - 1 - DEC_BATCH, D_MODEL), F32)], axis=0)
    mod = _modulation(cond, w_mod, b_mod).reshape(DEPTH, N_COND, N_MOD, D_MODEL)
    norm_g4 = norm_g.reshape(DEPTH, 3, 1, D_MODEL)
    final_g2 = final_g.reshape(1, D_MODEL)
    rope = _rope_tables()
    inner = N_HEADS * HEAD_DIM

    new_sa, new_k, new_v = [], [], []
    for i in range(DEPTH):
        kind, slot = i % 3, i // 3
        x = _half_ffn(x, mod, norm_g4, ffn_wg, ffn_wu, ffn_wd, final_g2, i, 0, 0, False)
        if kind == 0:
            w_in = a_w_in[slot]
            proj = _norm_proj(x, mod, norm_g4, w_in[:, :4 * inner], i, 0, N_ROWS, 512)
            gates = _gate_proj(x, mod, norm_g4, w_in[:, 4 * inner:].T, i)
            gates_t = gates.reshape(2, 2, N_HEADS, N_ROWS).transpose(2, 0, 1, 3).reshape(N_HEADS, 4, N_ROWS)
            zero = jnp.zeros((2, N_HEADS), F32)
            par = jnp.stack([jnp.stack([a_A_log[slot], zero], axis=1).reshape(4, N_HEADS),
                             jnp.stack([a_dt_bias[slot], zero], axis=1).reshape(4, N_HEADS)], axis=-1)
            par = par.transpose(1, 0, 2)
            ng = a_norm_g[slot].reshape(1, HEAD_DIM)
            o_p, s_new = _delta_core(proj, gates_t, a_conv[slot], par, ng, None, 0, BATCH, SEQ)
            o_s, _ = _delta_core(proj, gates_t, a_conv[slot], par, ng, state_a[:, slot], P_ROWS,
                                 DEC_BATCH, DEC_SEQ)
            new_sa.append(s_new)
            x = _out_proj(jnp.concatenate([o_p, o_s], axis=0), a_w_out[slot], x, mod, i)
        elif kind == 1:
            lam_init = 0.8 - 0.6 * math.exp(-0.3 * i)
            ng = b_norm_g[slot].reshape(1, HEAD_DIM)
            qkv_p = _norm_proj(x, mod, norm_g4, b_w_qkv[slot], i, 0, P_ROWS, 512)
            qkv_s = _norm_proj(x, mod, norm_g4, b_w_qkv[slot], i, P_ROWS, S_ROWS, 512, rope=rope)
            o_p = _attn_core(qkv_p, b_lam[slot], ng, lam_init, BATCH, SEQ, SEQ)
            o_s = _attn_core(qkv_s, b_lam[slot], ng, lam_init, DEC_BATCH, DEC_SEQ, 256,
                             cache_k[:, slot].reshape(DEC_BATCH, PAST_LEN, inner),
                             cache_v[:, slot].reshape(DEC_BATCH, PAST_LEN, inner))
            new_k.append(qkv_p[:, inner:2 * inner].reshape(BATCH, SEQ, N_HEADS, HEAD_DIM))
            new_v.append(qkv_p[:, 2 * inner:].reshape(BATCH, SEQ, N_HEADS, HEAD_DIM))
            x = _out_proj(jnp.concatenate([o_p, o_s], axis=0), b_w_out[slot], x, mod, i)
        else:
            hn = _norm_only(x, mod, norm_g4, i)
            sc = c_scale[slot].reshape(1, D_MODEL)
            x_p = _pool_mix(hn, x, c_w_pool[slot], sc, mod, i, 0, BATCH, SEQ)
            x_s = _pool_mix(hn, x, c_w_pool[slot], sc, mod, i, P_ROWS, DEC_BATCH, DEC_SEQ)
            x = jnp.concatenate([x_p, x_s], axis=0)
        x = _half_ffn(x, mod, norm_g4, ffn_wg, ffn_wu, ffn_wd, final_g2, i, 1, 6, i == DEPTH - 1)

    y_prompt = x[:P_ROWS].reshape(BATCH, SEQ, D_MODEL)
    y_sample = x[P_ROWS:].reshape(DEC_BATCH, DEC_SEQ, D_MODEL)
    return (y_prompt, y_sample, jnp.stack(new_sa, axis=1), jnp.stack(new_k, axis=1), jnp.stack(new_v, axis=1))
```
